```python
import jax, jax.numpy as jnp
from jax import lax
import numpy as np

D_MODEL = 1024
BATCH = 4
SEQ = 8192
DEPTH = 2

MEM_LEN = 256
D_FF = 2816
N_EVEN = (DEPTH + 1) // 2
N_ODD = DEPTH // 2

CONV_A_CH = 512
CONV_A_WIDTH = 31
SWA_HEADS = 8
SWA_KV_HEADS = 2
SWA_GROUP = SWA_HEADS // SWA_KV_HEADS
HEAD_DIM = 64
WINDOW = 128
BLOCK = 128
EVEN_IN = 2 * CONV_A_CH + (SWA_HEADS + 2 * SWA_KV_HEADS) * HEAD_DIM
EVEN_MIX = CONV_A_CH + SWA_HEADS * HEAD_DIM
SC_CH = 1024
SC_WIDTH = 3
XA_HEADS = 4
XA_HEAD_DIM = D_MODEL // XA_HEADS

RMS_EPS = 1e-6
LN_EPS = 1e-5

kernel_name = "hybrid_conformer_swa_shortconv_macaron"


def rmsnorm(x, g):
    x32 = x.astype(jnp.float32)
    y = x32 * lax.rsqrt(jnp.mean(x32 * x32, axis=-1, keepdims=True) + RMS_EPS)
    return y.astype(x.dtype) * g


def layernorm(x, g, b):
    x32 = x.astype(jnp.float32)
    mu = jnp.mean(x32, axis=-1, keepdims=True)
    var = jnp.mean(jnp.square(x32 - mu), axis=-1, keepdims=True)
    y = (x32 - mu) * lax.rsqrt(var + LN_EPS)
    return y.astype(x.dtype) * g + b


def swiglu(u, w_gu, w_down):
    gu = u @ w_gu
    return (jax.nn.silu(gu[..., :D_FF]) * gu[..., D_FF:]) @ w_down


def causal_depthwise_conv(x, w):
    k_width, ch = w.shape
    return lax.conv_general_dilated(
        x, w[:, None, :].astype(x.dtype), window_strides=(1,), padding=[(k_width - 1, 0)],
        dimension_numbers=("NWC", "WIO", "NWC"), feature_group_count=ch)


def alibi_slopes(n_heads):
    return 2.0 ** (-8.0 * jnp.arange(1, n_heads + 1, dtype=jnp.float32) / n_heads)


def conformer_conv(a_val, a_gate, conv_w, conv_b, ln_g, ln_b):
    a = a_val * jax.nn.sigmoid(a_gate)
    a = causal_depthwise_conv(a, conv_w) + conv_b
    return jax.nn.silu(layernorm(a, ln_g, ln_b))


def sliding_window_gqa(q, k, v, sinks):
    bsz, seq = q.shape[:2]
    nb = seq // BLOCK
    qb = q.reshape(bsz, nb, BLOCK, SWA_KV_HEADS, SWA_GROUP, HEAD_DIM)
    kb = k.reshape(bsz, nb, BLOCK, SWA_KV_HEADS, HEAD_DIM)
    vb = v.reshape(bsz, nb, BLOCK, SWA_KV_HEADS, HEAD_DIM)
    pad = ((0, 0), (1, 0), (0, 0), (0, 0), (0, 0))
    kk = jnp.concatenate([jnp.pad(kb, pad)[:, :-1], kb], axis=2)
    vv = jnp.concatenate([jnp.pad(vb, pad)[:, :-1], vb], axis=2)
    scores = jnp.einsum("bnqkgd,bnskd->bnkgqs", qb, kk).astype(jnp.float32) * (HEAD_DIM ** -0.5)
    dist = jnp.arange(BLOCK)[:, None] + BLOCK - jnp.arange(2 * BLOCK)[None, :]
    key_pos = jnp.arange(nb)[:, None] * BLOCK - BLOCK + jnp.arange(2 * BLOCK)[None, :]
    valid = ((dist >= 0) & (dist < WINDOW))[None] & (key_pos >= 0)[:, None, :]
    slopes = alibi_slopes(SWA_HEADS).reshape(SWA_KV_HEADS, SWA_GROUP)
    scores = scores - slopes[:, :, None, None] * dist.astype(jnp.float32)
    scores = jnp.where(valid[None, :, None, None], scores, -jnp.inf)
    sink = jnp.broadcast_to(
        sinks.astype(jnp.float32).reshape(SWA_KV_HEADS, SWA_GROUP)[None, None, :, :, None, None],
        scores.shape[:-1] + (1,))
    probs = jax.nn.softmax(jnp.concatenate([scores, sink], axis=-1), axis=-1)[..., :-1]
    out = jnp.einsum("bnkgqs,bnskd->bnqkgd", probs.astype(vv.dtype), vv)
    return out.reshape(bsz, seq, SWA_HEADS * HEAD_DIM)


def even_mixer(u, w_in, conv_w, conv_b, ln_g, ln_b, sinks, w_out):
    bsz, seq, _ = u.shape
    z = u @ w_in
    o0 = CONV_A_CH
    o1 = o0 + CONV_A_CH
    o2 = o1 + SWA_HEADS * HEAD_DIM
    o3 = o2 + SWA_KV_HEADS * HEAD_DIM
    a = conformer_conv(z[..., :o0], z[..., o0:o1], conv_w, conv_b, ln_g, ln_b)
    q = z[..., o1:o2].reshape(bsz, seq, SWA_KV_HEADS, SWA_GROUP, HEAD_DIM)
    k = z[..., o2:o3].reshape(bsz, seq, SWA_KV_HEADS, HEAD_DIM)
    v = z[..., o3:].reshape(bsz, seq, SWA_KV_HEADS, HEAD_DIM)
    o = sliding_window_gqa(q, k, v, sinks)
    return jnp.concatenate([a, o], axis=-1) @ w_out


def odd_mixer(u, w_in, conv_w, w_out):
    z = u @ w_in
    gate_b = z[..., :SC_CH]
    gate_c = z[..., SC_CH:2 * SC_CH]
    val = z[..., 2 * SC_CH:]
    y = gate_b * causal_depthwise_conv(gate_c * val, conv_w)
    return y @ w_out


def memory_cross_attention(u, m, wq, wkv, wo):
    bsz, seq, _ = u.shape
    mlen = m.shape[1]
    q = (u @ wq).reshape(bsz, seq, XA_HEADS, XA_HEAD_DIM)
    kv = m @ wkv
    k = kv[..., :D_MODEL].reshape(bsz, mlen, XA_HEADS, XA_HEAD_DIM)
    v = kv[..., D_MODEL:].reshape(bsz, mlen, XA_HEADS, XA_HEAD_DIM)
    s = jnp.einsum("bqhd,bkhd->bhqk", q, k).astype(jnp.float32) * (XA_HEAD_DIM ** -0.5)
    p = jax.nn.softmax(s, axis=-1).astype(v.dtype)
    o = jnp.einsum("bhqk,bkhd->bqhd", p, v).reshape(bsz, seq, D_MODEL)
    return o @ wo


def _normal(key, shape, fan_in):
    return jax.random.normal(key, shape, jnp.float32) * (fan_in ** -0.5)


def _gain(key, shape):
    return 1.0 + 0.05 * jax.random.normal(key, shape, jnp.float32)


def setup_inputs(seed: int = 0) -> dict:
    key = jax.random.key(seed)
    ks = jax.random.split(key, 32)
    D, F = D_MODEL, D_FF
    return {
        "x": jax.random.normal(ks[0], (BATCH, SEQ, D), jnp.float32),
        "mem": jax.random.normal(ks[1], (BATCH, MEM_LEN, D), jnp.float32),
        "ffn1_norm": _gain(ks[2], (DEPTH, D)),
        "ffn1_w_gu": _normal(ks[3], (DEPTH, D, 2 * F), D),
        "ffn1_w_down": _normal(ks[4], (DEPTH, F, D), F),
        "mix_norm": _gain(ks[5], (DEPTH, D)),
        "even_w_in": _normal(ks[6], (N_EVEN, D, EVEN_IN), D),
        "conv_a_w": _normal(ks[7], (N_EVEN, CONV_A_WIDTH, CONV_A_CH), CONV_A_WIDTH),
        "conv_a_b": 0.02 * jax.random.normal(ks[8], (N_EVEN, CONV_A_CH), jnp.float32),
        "conv_a_ln_g": _gain(ks[9], (N_EVEN, CONV_A_CH)),
        "conv_a_ln_b": 0.02 * jax.random.normal(ks[10], (N_EVEN, CONV_A_CH), jnp.float32),
        "swa_sinks": 0.5 * jax.random.normal(ks[11], (N_EVEN, SWA_HEADS), jnp.float32),
        "even_w_out": _normal(ks[12], (N_EVEN, EVEN_MIX, D), EVEN_MIX),
        "odd_w_in": _normal(ks[13], (N_ODD, D, 3 * SC_CH), D),
        "sc_conv_w": _normal(ks[14], (N_ODD, SC_WIDTH, SC_CH), SC_WIDTH),
        "odd_w_out": _normal(ks[15], (N_ODD, SC_CH, D), SC_CH),
        "xa_norm": _gain(ks[16], (DEPTH, D)),
        "xa_mem_norm": _gain(ks[17], (DEPTH, D)),
        "xa_wq": _normal(ks[18], (DEPTH, D, D), D),
        "xa_wkv": _normal(ks[19], (DEPTH, D, 2 * D), D),
        "xa_wo": _normal(ks[20], (DEPTH, D, D), D),
        "ffn2_norm": _gain(ks[21], (DEPTH, D)),
        "ffn2_w_gu": _normal(ks[22], (DEPTH, D, 2 * F), D),
        "ffn2_w_down": _normal(ks[23], (DEPTH, F, D), F),
        "final_norm": _gain(ks[24], (D,)),
    }


def reference(x, mem, ffn1_norm, ffn1_w_gu, ffn1_w_down, mix_norm, even_w_in, conv_a_w, conv_a_b,
              conv_a_ln_g, conv_a_ln_b, swa_sinks, even_w_out, odd_w_in, sc_conv_w, odd_w_out,
              xa_norm, xa_mem_norm, xa_wq, xa_wkv, xa_wo, ffn2_norm, ffn2_w_gu, ffn2_w_down, final_norm):
    h = x
    for i in range(DEPTH):
        h = h + 0.5 * swiglu(rmsnorm(h, ffn1_norm[i]), ffn1_w_gu[i], ffn1_w_down[i])
        u = rmsnorm(h, mix_norm[i])
        j = i // 2
        if i % 2 == 0:
            h = h + even_mixer(u, even_w_in[j], conv_a_w[j], conv_a_b[j], conv_a_ln_g[j],
                               conv_a_ln_b[j], swa_sinks[j], even_w_out[j])
        else:
            h = h + odd_mixer(u, odd_w_in[j], sc_conv_w[j], odd_w_out[j])
        h = h + memory_cross_attention(rmsnorm(h, xa_norm[i]), rmsnorm(mem, xa_mem_norm[i]),
                                       xa_wq[i], xa_wkv[i], xa_wo[i])
        h = h + 0.5 * swiglu(rmsnorm(h, ffn2_norm[i]), ffn2_w_gu[i], ffn2_w_down[i])
    return rmsnorm(h, final_norm)
```

```python
import functools

import jax
import jax.numpy as jnp
from jax import lax
from jax.experimental import pallas as pl
from jax.experimental.pallas import tpu as pltpu

D_MODEL = 1024
D_FF = 2816
CONV_A_CH = 512
CONV_A_WIDTH = 31
SWA_HEADS = 8
SWA_KV_HEADS = 2
HEAD_DIM = 64
BLOCK = 128
EVEN_IN = 1792
SC_CH = 1024
SC_WIDTH = 3
XA_HEADS = 4
XA_HEAD_DIM = 256
MEM_LEN = 256
RMS_EPS = 1e-6
LN_EPS = 1e-5

LANES = 128
SUBLANES = 8
VMEM_LIMIT_BYTES = 56 * 1024 * 1024

TOKEN_TILE = 512
FF_CHUNK = 512
CONV_ROWS = 64
CONV_A_HALO = 32
SC_HALO = 8

F32 = jnp.float32
BF16 = jnp.bfloat16
_NT = (((1,), (1,)), ((), ()))


def _resident(shape):
    zeros = (0,) * len(shape)
    return pl.BlockSpec(shape, lambda *_: zeros, pipeline_mode=pl.Buffered(1))


def _params(n_axes):
    return pltpu.CompilerParams(
        dimension_semantics=("arbitrary",) * n_axes,
        vmem_limit_bytes=VMEM_LIMIT_BYTES)


def _rmsnorm(x, g):
    ms = jnp.mean(x * x, axis=-1, keepdims=True)
    return (x * lax.rsqrt(ms + RMS_EPS)) * g


def _dot(a, b):
    return jnp.dot(a, b, preferred_element_type=F32)


def _ffn_kernel(*refs, final):
    if final:
        h_ref, g_ref, wgu_ref, wd_ref, fg_ref, o_ref = refs
    else:
        h_ref, g_ref, wgu_ref, wd_ref, o_ref = refs
    x = h_ref[...]
    u = _rmsnorm(x, g_ref[...]).astype(BF16)
    acc = None
    for c0 in range(0, D_FF, FF_CHUNK):
        c1 = min(c0 + FF_CHUNK, D_FF)
        gate = _dot(u, wgu_ref[:, c0:c1])
        up = _dot(u, wgu_ref[:, D_FF + c0:D_FF + c1])
        a = (jax.nn.silu(gate) * up).astype(BF16)
        y = _dot(a, wd_ref[c0:c1, :])
        acc = y if acc is None else acc + y
    out = x + 0.5 * acc
    if final:
        out = _rmsnorm(out, fg_ref[...])
    o_ref[...] = out


def _ffn(h, g, wgu, wd, final_g=None):
    t = h.shape[0]
    final = final_g is not None
    row = pl.BlockSpec((TOKEN_TILE, D_MODEL), lambda i: (i, 0))
    in_specs = [row, _resident((1, D_MODEL)), _resident((D_MODEL, 2 * D_FF)),
                _resident((D_FF, D_MODEL))]
    args = [h, g.reshape(1, D_MODEL), wgu, wd]
    if final:
        in_specs.append(_resident((1, D_MODEL)))
        args.append(final_g.reshape(1, D_MODEL))
    return pl.pallas_call(
        functools.partial(_ffn_kernel, final=final),
        grid=(t // TOKEN_TILE,),
        in_specs=in_specs,
        out_specs=row,
        out_shape=jax.ShapeDtypeStruct((t, D_MODEL), F32),
        compiler_params=_params(1),
        name="ffn_final" if final else "ffn",
    )(*args)


def _even_kernel(h_ref, g_ref, win_ref, cw_ref, cb_ref, lng_ref, lnb_ref, sink_ref, bias_ref,
                 wout_ref, o_ref, cbuf, kvbuf, qbuf, mix):
    j = pl.program_id(1)
    tm = TOKEN_TILE

    @pl.when(j == 0)
    def _():
        cbuf[0:CONV_A_HALO, :] = jnp.zeros((CONV_A_HALO, CONV_A_CH), F32)
        kvbuf[0:BLOCK, :] = jnp.zeros((BLOCK, 8 * LANES), BF16)

    x = h_ref[...]
    u = _rmsnorm(x, g_ref[...]).astype(BF16)
    z = _dot(u, win_ref[...])

    cbuf[CONV_A_HALO:CONV_A_HALO + tm, :] = z[:, 0:512] * jax.nn.sigmoid(z[:, 512:1024])
    qbuf[...] = (z[:, 1024:1536] * (HEAD_DIM ** -0.5)).astype(BF16)

    lo = lax.broadcasted_iota(jnp.int32, (tm, LANES), 1) < HEAD_DIM
    for base, col in ((0, 1536), (4, 1664)):
        f = z[:, col:col + LANES]
        r = pltpu.roll(f, HEAD_DIM, 1)
        parts = (jnp.where(lo, f, 0.0), jnp.where(lo, 0.0, r),
                 jnp.where(lo, r, 0.0), jnp.where(lo, 0.0, f))
        for n, part in enumerate(parts):
            kvbuf[BLOCK:BLOCK + tm, (base + n) * LANES:(base + n + 1) * LANES] = part.astype(BF16)

    cb = cb_ref[...]
    lng = lng_ref[...]
    lnb = lnb_ref[...]
    for r0 in range(0, tm, CONV_ROWS):
        acc = jnp.broadcast_to(cb, (CONV_ROWS, CONV_A_CH))
        for d in range(CONV_A_WIDTH):
            k = CONV_A_WIDTH - 1 - d
            s0 = CONV_A_HALO + r0 - d
            acc = acc + cw_ref[k:k + 1, :] * cbuf[s0:s0 + CONV_ROWS, :]
        mu = jnp.mean(acc, axis=-1, keepdims=True)
        cen = acc - mu
        var = jnp.mean(cen * cen, axis=-1, keepdims=True)
        y = cen * lax.rsqrt(var + LN_EPS) * lng + lnb
        mix[r0:r0 + CONV_ROWS, 0:CONV_A_CH] = jax.nn.silu(y).astype(BF16)

    lo_q = lax.broadcasted_iota(jnp.int32, (BLOCK, LANES), 1) < HEAD_DIM
    first = jnp.where(j == 0, SWA_HEADS, 0)
    for blk in range(tm // BLOCK):
        r0 = blk * BLOCK
        for pr in range(SWA_HEADS // 2):
            kvh = pr // (SWA_HEADS // SWA_KV_HEADS // 2)
            qp = qbuf[r0:r0 + BLOCK, pr * LANES:(pr + 1) * LANES]
            probs = []
            inv = []
            for pos in range(2):
                h = 2 * pr + pos
                c = (2 * kvh + pos) * LANES
                kw = kvbuf[r0:r0 + 2 * BLOCK, c:c + LANES]
                bias = bias_ref[first + h] if blk == 0 else bias_ref[h]
                s = lax.dot_general(qp, kw, _NT, preferred_element_type=F32) + bias
                sk = sink_ref[h:h + 1, 0:1]
                m = jnp.maximum(jnp.max(s, axis=-1, keepdims=True), sk)
                p = jnp.exp(s - m)
                l = jnp.sum(p, axis=-1, keepdims=True) + jnp.exp(sk - m)
                probs.append(p.astype(BF16))
                inv.append(1.0 / l)
            c = (4 + 2 * kvh) * LANES
            vcat = jnp.concatenate([kvbuf[r0:r0 + 2 * BLOCK, c:c + LANES],
                                    kvbuf[r0:r0 + 2 * BLOCK, c + LANES:c + 2 * LANES]], axis=0)
            o = _dot(jnp.concatenate(probs, axis=-1), vcat)
            o = o * jnp.where(lo_q, inv[0], inv[1])
            mix[r0:r0 + BLOCK, CONV_A_CH + pr * LANES:CONV_A_CH + (pr + 1) * LANES] = o.astype(BF16)

    cbuf[0:CONV_A_HALO, :] = cbuf[tm:tm + CONV_A_HALO, :]
    kvbuf[0:BLOCK, :] = kvbuf[tm:tm + BLOCK, :]

    o_ref[...] = x + _dot(mix[...], wout_ref[...])


def _swa_bias():
    i = jnp.arange(BLOCK)[:, None]
    jj = jnp.arange(2 * BLOCK)[None, :]
    dist = i + BLOCK - jj
    valid = (dist >= 0) & (dist < BLOCK)
    slopes = 2.0 ** (-8.0 * jnp.arange(1, SWA_HEADS + 1, dtype=F32) / SWA_HEADS)
    b = -slopes[:, None, None] * dist.astype(F32)[None]
    normal = jnp.where(valid[None], b, -jnp.inf)
    first = jnp.where((valid & (jj >= BLOCK))[None], b, -jnp.inf)
    return jnp.concatenate([normal, first], axis=0)


def _even_mixer(h, g, w_in, cw, cb, lng, lnb, sinks, w_out, batch):
    t = h.shape[0]
    nj = t // batch // TOKEN_TILE
    row = pl.BlockSpec((TOKEN_TILE, D_MODEL), lambda b, j: (b * nj + j, 0))
    return pl.pallas_call(
        _even_kernel,
        grid=(batch, nj),
        in_specs=[row, _resident((1, D_MODEL)), _resident((D_MODEL, EVEN_IN)),
                  _resident((CONV_A_WIDTH, CONV_A_CH)), _resident((1, CONV_A_CH)),
                  _resident((1, CONV_A_CH)), _resident((1, CONV_A_CH)),
                  _resident((SWA_HEADS, LANES)), _resident((2 * SWA_HEADS, BLOCK, 2 * BLOCK)),
                  _resident((D_MODEL, D_MODEL))],
        out_specs=row,
        out_shape=jax.ShapeDtypeStruct((t, D_MODEL), F32),
        scratch_shapes=[pltpu.VMEM((CONV_A_HALO + TOKEN_TILE, CONV_A_CH), F32),
                        pltpu.VMEM((BLOCK + TOKEN_TILE, 8 * LANES), BF16),
                        pltpu.VMEM((TOKEN_TILE, SWA_HEADS * HEAD_DIM), BF16),
                        pltpu.VMEM((TOKEN_TILE, D_MODEL), BF16)],
        compiler_params=_params(2),
        name="even_mixer",
    )(h, g.reshape(1, D_MODEL), w_in, cw, cb.reshape(1, -1), lng.reshape(1, -1), lnb.reshape(1, -1),
      jnp.broadcast_to(sinks[:, None], (SWA_HEADS, LANES)), _swa_bias(), w_out)


def _odd_kernel(h_ref, g_ref, win_ref, cw_ref, wout_ref, o_ref, cbuf):
    j = pl.program_id(1)
    tm = TOKEN_TILE

    @pl.when(j == 0)
    def _():
        cbuf[0:SC_HALO, :] = jnp.zeros((SC_HALO, SC_CH), F32)

    x = h_ref[...]
    u = _rmsnorm(x, g_ref[...]).astype(BF16)
    z = _dot(u, win_ref[...])
    cbuf[SC_HALO:SC_HALO + tm, :] = z[:, SC_CH:2 * SC_CH] * z[:, 2 * SC_CH:3 * SC_CH]
    y = None
    for d in range(SC_WIDTH):
        k = SC_WIDTH - 1 - d
        term = cw_ref[k:k + 1, :] * cbuf[SC_HALO - d:SC_HALO - d + tm, :]
        y = term if y is None else y + term
    cbuf[0:SC_HALO, :] = cbuf[tm:tm + SC_HALO, :]
    o_ref[...] = x + _dot((z[:, 0:SC_CH] * y).astype(BF16), wout_ref[...])


def _odd_mixer(h, g, w_in, cw, w_out, batch):
    t = h.shape[0]
    nj = t // batch // TOKEN_TILE
    row = pl.BlockSpec((TOKEN_TILE, D_MODEL), lambda b, j: (b * nj + j, 0))
    return pl.pallas_call(
        _odd_kernel,
        grid=(batch, nj),
        in_specs=[row, _resident((1, D_MODEL)), _resident((D_MODEL, 3 * SC_CH)),
                  _resident((SC_WIDTH, SC_CH)), _resident((SC_CH, D_MODEL))],
        out_specs=row,
        out_shape=jax.ShapeDtypeStruct((t, D_MODEL), F32),
        scratch_shapes=[pltpu.VMEM((SC_HALO + TOKEN_TILE, SC_CH), F32)],
        compiler_params=_params(2),
        name="odd_mixer",
    )(h, g.reshape(1, D_MODEL), w_in, cw, w_out)


def _memkv_kernel(m_ref, g_ref, wkv_ref, o_ref):
    o_ref[...] = _dot(_rmsnorm(m_ref[...], g_ref[...]).astype(BF16), wkv_ref[...]).astype(BF16)


def _mem_kv(mem2d, g, wkv):
    rows = mem2d.shape[0]
    return pl.pallas_call(
        _memkv_kernel,
        grid=(rows // MEM_LEN,),
        in_specs=[pl.BlockSpec((MEM_LEN, D_MODEL), lambda i: (i, 0)), _resident((1, D_MODEL)),
                  _resident((D_MODEL, 2 * D_MODEL))],
        out_specs=pl.BlockSpec((MEM_LEN, 2 * D_MODEL), lambda i: (i, 0)),
        out_shape=jax.ShapeDtypeStruct((rows, 2 * D_MODEL), BF16),
        compiler_params=_params(1),
        name="mem_kv",
    )(mem2d, g.reshape(1, D_MODEL), wkv)


def _xa_kernel(h_ref, g_ref, wq_ref, kv_ref, wo_ref, o_ref):
    x = h_ref[...]
    u = _rmsnorm(x, g_ref[...]).astype(BF16)
    q = (_dot(u, wq_ref[...]) * (XA_HEAD_DIM ** -0.5)).astype(BF16)
    outs = []
    for h in range(XA_HEADS):
        c = h * XA_HEAD_DIM
        s = lax.dot_general(q[:, c:c + XA_HEAD_DIM], kv_ref[:, c:c + XA_HEAD_DIM], _NT,
                            preferred_element_type=F32)
        m = jnp.max(s, axis=-1, keepdims=True)
        p = jnp.exp(s - m)
        l = jnp.sum(p, axis=-1, keepdims=True)
        o = _dot(p.astype(BF16), kv_ref[:, D_MODEL + c:D_MODEL + c + XA_HEAD_DIM])
        outs.append((o * (1.0 / l)).astype(BF16))
    o_ref[...] = x + _dot(jnp.concatenate(outs, axis=-1), wo_ref[...])


def _cross_attention(h, g, wq, kv, wo, batch):
    t = h.shape[0]
    nj = t // batch // TOKEN_TILE
    row = pl.BlockSpec((TOKEN_TILE, D_MODEL), lambda b, j: (b * nj + j, 0))
    return pl.pallas_call(
        _xa_kernel,
        grid=(batch, nj),
        in_specs=[row, _resident((1, D_MODEL)), _resident((D_MODEL, D_MODEL)),
                  pl.BlockSpec((MEM_LEN, 2 * D_MODEL), lambda b, j: (b, 0)),
                  _resident((D_MODEL, D_MODEL))],
        out_specs=row,
        out_shape=jax.ShapeDtypeStruct((t, D_MODEL), F32),
        compiler_params=_params(2),
        name="cross_attention",
    )(h, g.reshape(1, D_MODEL), wq, kv, wo)


def kernel(x, mem, ffn1_norm, ffn1_w_gu, ffn1_w_down, mix_norm, even_w_in, conv_a_w, conv_a_b,
           conv_a_ln_g, conv_a_ln_b, swa_sinks, even_w_out, odd_w_in, sc_conv_w, odd_w_out,
           xa_norm, xa_mem_norm, xa_wq, xa_wkv, xa_wo, ffn2_norm, ffn2_w_gu, ffn2_w_down, final_norm):
    batch, seq, d = x.shape
    depth = ffn1_norm.shape[0]
    assert d == D_MODEL and seq % TOKEN_TILE == 0 and TOKEN_TILE % BLOCK == 0
    assert mem.shape[1] == MEM_LEN
    h = x.reshape(batch * seq, d)
    mem2d = mem.reshape(batch * MEM_LEN, d)
    bf = lambda w: w.astype(BF16)
    for i in range(depth):
        j = i // 2
        h = _ffn(h, ffn1_norm[i], bf(ffn1_w_gu[i]), bf(ffn1_w_down[i]))
        if i % 2 == 0:
            h = _even_mixer(h, mix_norm[i], bf(even_w_in[j]), conv_a_w[j], conv_a_b[j], conv_a_ln_g[j],
                            conv_a_ln_b[j], swa_sinks[j], bf(even_w_out[j]), batch)
        else:
            h = _odd_mixer(h, mix_norm[i], bf(odd_w_in[j]), sc_conv_w[j], bf(odd_w_out[j]), batch)
        kv = _mem_kv(mem2d, xa_mem_norm[i], bf(xa_wkv[i]))
        h = _cross_attention(h, xa_norm[i], bf(xa_wq[i]), kv, bf(xa_wo[i]), batch)
        h = _ffn(h, ffn2_norm[i], bf(ffn2_w_gu[i]), bf(ffn2_w_down[i]),
                 final_g=final_norm if i == depth - 1 else None)
    return h.reshape(batch, seq, d)
```

```python
import functools

import jax
import jax.numpy as jnp
from jax import lax
from jax.experimental import pallas as pl
from jax.experimental.pallas import tpu as pltpu

D_MODEL = 1024
D_FF = 2816
CONV_A_CH = 512
CONV_A_WIDTH = 31
SWA_HEADS = 8
SWA_KV_HEADS = 2
HEAD_DIM = 64
BLOCK = 128
EVEN_IN = 1792
SC_CH = 1024
SC_WIDTH = 3
XA_HEADS = 4
XA_HEAD_DIM = 256
MEM_LEN = 256
RMS_EPS = 1e-6
LN_EPS = 1e-5

LANES = 128
SUBLANES = 8
VMEM_LIMIT_BYTES = 56 * 1024 * 1024

TOKEN_TILE = 512
SUB_TILE = 256
FF_CHUNK = 512
CONV_ROWS = 64
CONV_A_HALO = 32
SC_HALO = 8

F32 = jnp.float32
BF16 = jnp.bfloat16
_NT = (((1,), (1,)), ((), ()))


def _resident(shape):
    zeros = (0,) * len(shape)
    return pl.BlockSpec(shape, lambda *_: zeros, pipeline_mode=pl.Buffered(1))


def _layer(shape, layer):
    index = (layer,) + (0,) * len(shape)
    return pl.BlockSpec((None,) + shape, lambda *_: index, pipeline_mode=pl.Buffered(1))


def _rows(p):
    return p.reshape(p.shape[0], 1, p.shape[1])


def _params(n_axes):
    return pltpu.CompilerParams(
        dimension_semantics=("arbitrary",) * n_axes,
        vmem_limit_bytes=VMEM_LIMIT_BYTES)


def _rmsnorm(x, g):
    ms = jnp.mean(x * x, axis=-1, keepdims=True)
    return (x * lax.rsqrt(ms + RMS_EPS)) * g


def _dot(a, b):
    return jnp.dot(a, b, preferred_element_type=F32)


def _ffn_kernel(*refs, final):
    if final:
        h_ref, g_ref, wgu_ref, wd_ref, fg_ref, o_ref = refs
    else:
        h_ref, g_ref, wgu_ref, wd_ref, o_ref = refs
    x = h_ref[...]
    u = _rmsnorm(x, g_ref[...]).astype(BF16)
    acc = None
    for c0 in range(0, D_FF, FF_CHUNK):
        c1 = min(c0 + FF_CHUNK, D_FF)
        gate = _dot(u, wgu_ref[:, c0:c1])
        up = _dot(u, wgu_ref[:, D_FF + c0:D_FF + c1])
        a = (jax.nn.silu(gate) * up).astype(BF16)
        y = _dot(a, wd_ref[c0:c1, :])
        acc = y if acc is None else acc + y
    out = x + 0.5 * acc
    if final:
        out = _rmsnorm(out, fg_ref[...])
    o_ref[...] = out


def _ffn(h, layer, g, wgu, wd, final_g=None):
    t = h.shape[0]
    final = final_g is not None
    row = pl.BlockSpec((TOKEN_TILE, D_MODEL), lambda i: (i, 0))
    in_specs = [row, _layer((1, D_MODEL), layer), _layer((D_MODEL, 2 * D_FF), layer),
                _layer((D_FF, D_MODEL), layer)]
    args = [h, _rows(g), wgu, wd]
    if final:
        in_specs.append(_resident((1, D_MODEL)))
        args.append(final_g.reshape(1, D_MODEL))
    return pl.pallas_call(
        functools.partial(_ffn_kernel, final=final),
        grid=(t // TOKEN_TILE,),
        in_specs=in_specs,
        out_specs=row,
        out_shape=jax.ShapeDtypeStruct((t, D_MODEL), F32),
        compiler_params=_params(1),
        name="ffn_final" if final else "ffn",
    )(*args)


def _even_kernel(h_ref, g_ref, win_ref, cw_ref, cb_ref, lng_ref, lnb_ref, sink_ref, bias_ref,
                 wout_ref, o_ref, cbuf, kvbuf, qbuf, mix):
    j = pl.program_id(1)
    tm = TOKEN_TILE

    @pl.when(j == 0)
    def _():
        cbuf[0:CONV_A_HALO, :] = jnp.zeros((CONV_A_HALO, CONV_A_CH), F32)
        kvbuf[0:BLOCK, :] = jnp.zeros((BLOCK, 8 * LANES), BF16)

    g = g_ref[...]
    lo = lax.broadcasted_iota(jnp.int32, (SUB_TILE, LANES), 1) < HEAD_DIM

    def in_proj(r0):
        u = _rmsnorm(h_ref[r0:r0 + SUB_TILE, :], g).astype(BF16)
        z = _dot(u, win_ref[...])
        cbuf[CONV_A_HALO + r0:CONV_A_HALO + r0 + SUB_TILE, :] = (
            z[:, 0:512] * jax.nn.sigmoid(z[:, 512:1024]))
        qbuf[r0:r0 + SUB_TILE, :] = (z[:, 1024:1536] * (HEAD_DIM ** -0.5)).astype(BF16)
        for base, col in ((0, 1536), (4, 1664)):
            f = z[:, col:col + LANES]
            r = pltpu.roll(f, HEAD_DIM, 1)
            parts = (jnp.where(lo, f, 0.0), jnp.where(lo, 0.0, r),
                     jnp.where(lo, r, 0.0), jnp.where(lo, 0.0, f))
            for n, part in enumerate(parts):
                kvbuf[BLOCK + r0:BLOCK + r0 + SUB_TILE,
                      (base + n) * LANES:(base + n + 1) * LANES] = part.astype(BF16)

    lo_q = lax.broadcasted_iota(jnp.int32, (BLOCK, LANES), 1) < HEAD_DIM
    first = jnp.where(j == 0, SWA_HEADS, 0)
    group_pairs = SWA_HEADS // SWA_KV_HEADS // 2

    ones_rows = lax.broadcasted_iota(jnp.int32, (4 * BLOCK, LANES), 0) < 2 * BLOCK
    ones_lanes = lax.broadcasted_iota(jnp.int32, (4 * BLOCK, LANES), 1) < HEAD_DIM
    row_sum_w = jnp.where(ones_rows == ones_lanes, 1.0, 0.0).astype(BF16)

    def scores(blk):
        r0 = blk * BLOCK
        out = []
        for pr in range(SWA_HEADS // 2):
            kvh = pr // group_pairs
            qp = qbuf[r0:r0 + BLOCK, pr * LANES:(pr + 1) * LANES]
            probs = []
            sink_p = []
            for pos in range(2):
                h = 2 * pr + pos
                c = (2 * kvh + pos) * LANES
                kw = kvbuf[r0:r0 + 2 * BLOCK, c:c + LANES]
                bias = bias_ref[first + h] if blk == 0 else bias_ref[h]
                s = lax.dot_general(qp, kw, _NT, preferred_element_type=F32) + bias
                sk = sink_ref[h:h + 1, 0:1]
                m = jnp.maximum(jnp.max(s, axis=-1, keepdims=True), sk)
                probs.append(jnp.exp(s - m).astype(BF16))
                sink_p.append(jnp.exp(sk - m))
            out.append((jnp.concatenate(probs, axis=-1), jnp.where(lo_q, sink_p[0], sink_p[1])))
        return out

    def pv(blk, pairs):
        r0 = blk * BLOCK
        for pr, (pcat, sink_term) in enumerate(pairs):
            c = (4 + 2 * (pr // group_pairs)) * LANES
            vcat = jnp.concatenate([kvbuf[r0:r0 + 2 * BLOCK, c:c + LANES],
                                    kvbuf[r0:r0 + 2 * BLOCK, c + LANES:c + 2 * LANES]], axis=0)
            ol = _dot(pcat, jnp.concatenate([vcat, row_sum_w], axis=1))
            o = ol[:, 0:LANES] / (ol[:, LANES:2 * LANES] + sink_term)
            mix[r0:r0 + BLOCK, CONV_A_CH + pr * LANES:CONV_A_CH + (pr + 1) * LANES] = o.astype(BF16)

    cb = cb_ref[...]
    lng = lng_ref[...]
    lnb = lnb_ref[...]
    n_q = CONV_A_HALO // SUBLANES

    def conv(r0):
        cols = []
        for c0 in range(0, CONV_A_CH, LANES):
            win = cbuf[r0:r0 + CONV_A_HALO + CONV_ROWS, c0:c0 + LANES]
            y = jnp.broadcast_to(cb[:, c0:c0 + LANES], (CONV_ROWS, LANES))
            for r in range(SUBLANES):
                part = None
                for q in range(n_q):
                    d = SUBLANES * q + r
                    if d >= CONV_A_WIDTH:
                        continue
                    k = CONV_A_WIDTH - 1 - d
                    s0 = CONV_A_HALO - SUBLANES * (q + 1)
                    term = cw_ref[k:k + 1, c0:c0 + LANES] * win[s0:s0 + SUBLANES + CONV_ROWS, :]
                    part = term if part is None else part + term
                y = y + part[SUBLANES - r:SUBLANES - r + CONV_ROWS, :]
            cols.append(y)
        acc = jnp.concatenate(cols, axis=-1)
        mu = jnp.mean(acc, axis=-1, keepdims=True)
        cen = acc - mu
        var = jnp.mean(cen * cen, axis=-1, keepdims=True)
        y = cen * lax.rsqrt(var + LN_EPS) * lng + lnb
        mix[r0:r0 + CONV_ROWS, 0:CONV_A_CH] = jax.nn.silu(y).astype(BF16)

    def out_proj(r0):
        o_ref[r0:r0 + SUB_TILE, :] = (h_ref[r0:r0 + SUB_TILE, :]
                                      + _dot(mix[r0:r0 + SUB_TILE, :], wout_ref[...]))

    for r0 in range(0, tm, SUB_TILE):
        in_proj(r0)
    n_blk = tm // BLOCK
    pending = None
    for blk in range(n_blk):
        cur = scores(blk)
        if blk > 0:
            pv(blk - 1, pending)
        pending = cur
    pv(n_blk - 1, pending)
    for r0 in range(0, tm, CONV_ROWS):
        conv(r0)
    for r0 in range(0, tm, SUB_TILE):
        out_proj(r0)

    cbuf[0:CONV_A_HALO, :] = cbuf[tm:tm + CONV_A_HALO, :]
    kvbuf[0:BLOCK, :] = kvbuf[tm:tm + BLOCK, :]


def _swa_bias():
    i = jnp.arange(BLOCK)[:, None]
    jj = jnp.arange(2 * BLOCK)[None, :]
    dist = i + BLOCK - jj
    valid = (dist >= 0) & (dist < BLOCK)
    slopes = 2.0 ** (-8.0 * jnp.arange(1, SWA_HEADS + 1, dtype=F32) / SWA_HEADS)
    b = -slopes[:, None, None] * dist.astype(F32)[None]
    normal = jnp.where(valid[None], b, -jnp.inf)
    first = jnp.where((valid & (jj >= BLOCK))[None], b, -jnp.inf)
    return jnp.concatenate([normal, first], axis=0)


def _even_mixer(h, layer, g, jl, w_in, cw, cb, lng, lnb, sinks, w_out, batch):
    t = h.shape[0]
    nj = t // batch // TOKEN_TILE
    row = pl.BlockSpec((TOKEN_TILE, D_MODEL), lambda b, j: (b * nj + j, 0))
    return pl.pallas_call(
        _even_kernel,
        grid=(batch, nj),
        in_specs=[row, _layer((1, D_MODEL), layer), _layer((D_MODEL, EVEN_IN), jl),
                  _layer((CONV_A_WIDTH, CONV_A_CH), jl), _layer((1, CONV_A_CH), jl),
                  _layer((1, CONV_A_CH), jl), _layer((1, CONV_A_CH), jl),
                  _layer((SWA_HEADS, LANES), jl), _resident((2 * SWA_HEADS, BLOCK, 2 * BLOCK)),
                  _layer((D_MODEL, D_MODEL), jl)],
        out_specs=row,
        out_shape=jax.ShapeDtypeStruct((t, D_MODEL), F32),
        scratch_shapes=[pltpu.VMEM((CONV_A_HALO + TOKEN_TILE, CONV_A_CH), F32),
                        pltpu.VMEM((BLOCK + TOKEN_TILE, 8 * LANES), BF16),
                        pltpu.VMEM((TOKEN_TILE, SWA_HEADS * HEAD_DIM), BF16),
                        pltpu.VMEM((TOKEN_TILE, D_MODEL), BF16)],
        compiler_params=_params(2),
        name="even_mixer",
    )(h, _rows(g), w_in, cw, _rows(cb), _rows(lng), _rows(lnb),
      jnp.broadcast_to(sinks[:, :, None], sinks.shape + (LANES,)), _swa_bias(), w_out)


def _odd_kernel(h_ref, g_ref, win_ref, cw_ref, wout_ref, o_ref, cbuf):
    j = pl.program_id(1)
    tm = TOKEN_TILE

    @pl.when(j == 0)
    def _():
        cbuf[0:SC_HALO, :] = jnp.zeros((SC_HALO, SC_CH), F32)

    x = h_ref[...]
    u = _rmsnorm(x, g_ref[...]).astype(BF16)
    z = _dot(u, win_ref[...])
    cbuf[SC_HALO:SC_HALO + tm, :] = z[:, SC_CH:2 * SC_CH] * z[:, 2 * SC_CH:3 * SC_CH]
    y = None
    for d in range(SC_WIDTH):
        k = SC_WIDTH - 1 - d
        term = cw_ref[k:k + 1, :] * cbuf[SC_HALO - d:SC_HALO - d + tm, :]
        y = term if y is None else y + term
    cbuf[0:SC_HALO, :] = cbuf[tm:tm + SC_HALO, :]
    o_ref[...] = x + _dot((z[:, 0:SC_CH] * y).astype(BF16), wout_ref[...])


def _odd_mixer(h, layer, g, jl, w_in, cw, w_out, batch):
    t = h.shape[0]
    nj = t // batch // TOKEN_TILE
    row = pl.BlockSpec((TOKEN_TILE, D_MODEL), lambda b, j: (b * nj + j, 0))
    return pl.pallas_call(
        _odd_kernel,
        grid=(batch, nj),
        in_specs=[row, _layer((1, D_MODEL), layer), _layer((D_MODEL, 3 * SC_CH), jl),
                  _layer((SC_WIDTH, SC_CH), jl), _layer((SC_CH, D_MODEL), jl)],
        out_specs=row,
        out_shape=jax.ShapeDtypeStruct((t, D_MODEL), F32),
        scratch_shapes=[pltpu.VMEM((SC_HALO + TOKEN_TILE, SC_CH), F32)],
        compiler_params=_params(2),
        name="odd_mixer",
    )(h, _rows(g), w_in, cw, w_out)


def _memkv_kernel(m_ref, g_ref, wkv_ref, o_ref):
    o_ref[...] = _dot(_rmsnorm(m_ref[...], g_ref[...]).astype(BF16), wkv_ref[...]).astype(BF16)


def _mem_kv(mem2d, g, wkv):
    rows = mem2d.shape[0]
    depth = wkv.shape[0]
    return pl.pallas_call(
        _memkv_kernel,
        grid=(depth, rows // MEM_LEN),
        in_specs=[pl.BlockSpec((MEM_LEN, D_MODEL), lambda l, b: (b, 0)),
                  pl.BlockSpec((None, 1, D_MODEL), lambda l, b: (l, 0, 0)),
                  pl.BlockSpec((None, D_MODEL, 2 * D_MODEL), lambda l, b: (l, 0, 0))],
        out_specs=pl.BlockSpec((None, MEM_LEN, 2 * D_MODEL), lambda l, b: (l, b, 0)),
        out_shape=jax.ShapeDtypeStruct((depth, rows, 2 * D_MODEL), BF16),
        compiler_params=_params(2),
        name="mem_kv",
    )(mem2d, _rows(g), wkv)


def _xa_kernel(h_ref, g_ref, wq_ref, kv_ref, wo_ref, o_ref):
    x = h_ref[...]
    u = _rmsnorm(x, g_ref[...]).astype(BF16)
    q = (_dot(u, wq_ref[...]) * (XA_HEAD_DIM ** -0.5)).astype(BF16)
    outs = []
    for h in range(XA_HEADS):
        c = h * XA_HEAD_DIM
        s = lax.dot_general(q[:, c:c + XA_HEAD_DIM], kv_ref[:, c:c + XA_HEAD_DIM], _NT,
                            preferred_element_type=F32)
        m = jnp.max(s, axis=-1, keepdims=True)
        p = jnp.exp(s - m)
        l = jnp.sum(p, axis=-1, keepdims=True)
        o = _dot(p.astype(BF16), kv_ref[:, D_MODEL + c:D_MODEL + c + XA_HEAD_DIM])
        outs.append((o * (1.0 / l)).astype(BF16))
    o_ref[...] = x + _dot(jnp.concatenate(outs, axis=-1), wo_ref[...])


def _cross_attention(h, layer, g, wq, kv, wo, batch):
    t = h.shape[0]
    nj = t // batch // TOKEN_TILE
    row = pl.BlockSpec((TOKEN_TILE, D_MODEL), lambda b, j: (b * nj + j, 0))
    return pl.pallas_call(
        _xa_kernel,
        grid=(batch, nj),
        in_specs=[row, _layer((1, D_MODEL), layer), _layer((D_MODEL, D_MODEL), layer),
                  pl.BlockSpec((None, MEM_LEN, 2 * D_MODEL), lambda b, j: (layer, b, 0)),
                  _layer((D_MODEL, D_MODEL), layer)],
        out_specs=row,
        out_shape=jax.ShapeDtypeStruct((t, D_MODEL), F32),
        compiler_params=_params(2),
        name="cross_attention",
    )(h, _rows(g), wq, kv, wo)


def kernel(x, mem, ffn1_norm, ffn1_w_gu, ffn1_w_down, mix_norm, even_w_in, conv_a_w, conv_a_b,
           conv_a_ln_g, conv_a_ln_b, swa_sinks, even_w_out, odd_w_in, sc_conv_w, odd_w_out,
           xa_norm, xa_mem_norm, xa_wq, xa_wkv, xa_wo, ffn2_norm, ffn2_w_gu, ffn2_w_down, final_norm):
    batch, seq, d = x.shape
    depth = ffn1_norm.shape[0]
    assert d == D_MODEL and seq % TOKEN_TILE == 0 and TOKEN_TILE % BLOCK == 0
    assert mem.shape[1] == MEM_LEN
    h = x.reshape(batch * seq, d)
    mem2d = mem.reshape(batch * MEM_LEN, d)
    bf = lambda w: w.astype(BF16)
    ffn1_w_gu, ffn1_w_down, ffn2_w_gu, ffn2_w_down = map(bf, (ffn1_w_gu, ffn1_w_down, ffn2_w_gu, ffn2_w_down))
    even_w_in, even_w_out, odd_w_in, odd_w_out = map(bf, (even_w_in, even_w_out, odd_w_in, odd_w_out))
    xa_wq, xa_wo = bf(xa_wq), bf(xa_wo)
    kv = _mem_kv(mem2d, xa_mem_norm, bf(xa_wkv))
    for i in range(depth):
        j = i // 2
        h = _ffn(h, i, ffn1_norm, ffn1_w_gu, ffn1_w_down)
        if i % 2 == 0:
            h = _even_mixer(h, i, mix_norm, j, even_w_in, conv_a_w, conv_a_b, conv_a_ln_g, conv_a_ln_b,
                            swa_sinks, even_w_out, batch)
        else:
            h = _odd_mixer(h, i, mix_norm, j, odd_w_in, sc_conv_w, odd_w_out, batch)
        h = _cross_attention(h, i, xa_norm, xa_wq, kv, xa_wo, batch)
        h = _ffn(h, i, ffn2_norm, ffn2_w_gu, ffn2_w_down,
                 final_g=final_norm if i == depth - 1 else None)
    return h.reshape(batch, seq, d)
```

```python
import functools

import jax
import jax.numpy as jnp
from jax import lax
from jax.experimental import pallas as pl
from jax.experimental.pallas import tpu as pltpu

D_MODEL = 1024
D_FF = 2816
CONV_A_CH = 512
CONV_A_WIDTH = 31
SWA_HEADS = 8
SWA_KV_HEADS = 2
HEAD_DIM = 64
BLOCK = 128
EVEN_IN = 1792
SC_CH = 1024
SC_WIDTH = 3
XA_HEADS = 4
XA_HEAD_DIM = 256
MEM_LEN = 256
RMS_EPS = 1e-6
LN_EPS = 1e-5

LANES = 128
SUBLANES = 8
VMEM_LIMIT_BYTES = 56 * 1024 * 1024

TOKEN_TILE = 512
SUB_TILE = 256
FFN_TILE = 1024
FFN_SUB = 512
FF_CHUNK = 512
CONV_ROWS = 64
CONV_A_HALO = 32
SC_HALO = 8

F32 = jnp.float32
BF16 = jnp.bfloat16
_NT = (((1,), (1,)), ((), ()))


def _resident(shape):
    zeros = (0,) * len(shape)
    return pl.BlockSpec(shape, lambda *_: zeros, pipeline_mode=pl.Buffered(1))


def _layer(shape, layer):
    index = (layer,) + (0,) * len(shape)
    return pl.BlockSpec((None,) + shape, lambda *_: index, pipeline_mode=pl.Buffered(1))


def _rows(p):
    return p.reshape(p.shape[0], 1, p.shape[1])


def _params(n_axes):
    return pltpu.CompilerParams(
        dimension_semantics=("arbitrary",) * n_axes,
        vmem_limit_bytes=VMEM_LIMIT_BYTES)


def _rmsnorm(x, g):
    ms = jnp.mean(x * x, axis=-1, keepdims=True)
    return (x * lax.rsqrt(ms + RMS_EPS)) * g


def _dot(a, b):
    return jnp.dot(a, b, preferred_element_type=F32)


def _ffn_kernel(*refs, final):
    if final:
        h_ref, g_ref, wgu_ref, wd_ref, fg_ref, o_ref = refs
    else:
        h_ref, g_ref, wgu_ref, wd_ref, o_ref = refs
    g = g_ref[...]

    def normed(r0):
        return _rmsnorm(h_ref[r0:r0 + FFN_SUB, :], g).astype(BF16)

    def half_step(r0, u):
        acc = None
        for c0 in range(0, D_FF, FF_CHUNK):
            c1 = min(c0 + FF_CHUNK, D_FF)
            gate = _dot(u, wgu_ref[:, c0:c1])
            up = _dot(u, wgu_ref[:, D_FF + c0:D_FF + c1])
            a = (jax.nn.silu(gate) * up).astype(BF16)
            y = _dot(a, wd_ref[c0:c1, :])
            acc = y if acc is None else acc + y
            yield
        out = h_ref[r0:r0 + FFN_SUB, :] + 0.5 * acc
        if final:
            out = _rmsnorm(out, fg_ref[...])
        o_ref[r0:r0 + FFN_SUB, :] = out
        yield

    u = normed(0)
    for r0 in range(0, FFN_TILE, FFN_SUB):
        steps = half_step(r0, u)
        next(steps)
        if r0 + FFN_SUB < FFN_TILE:
            u = normed(r0 + FFN_SUB)
        for _ in steps:
            pass


def _ffn(h, layer, g, wgu, wd, final_g=None):
    t = h.shape[0]
    final = final_g is not None
    row = pl.BlockSpec((FFN_TILE, D_MODEL), lambda i: (i, 0))
    in_specs = [row, _layer((1, D_MODEL), layer), _layer((D_MODEL, 2 * D_FF), layer),
                _layer((D_FF, D_MODEL), layer)]
    args = [h, _rows(g), wgu, wd]
    if final:
        in_specs.append(_resident((1, D_MODEL)))
        args.append(final_g.reshape(1, D_MODEL))
    return pl.pallas_call(
        functools.partial(_ffn_kernel, final=final),
        grid=(t // FFN_TILE,),
        in_specs=in_specs,
        out_specs=row,
        out_shape=jax.ShapeDtypeStruct((t, D_MODEL), F32),
        compiler_params=_params(1),
        name="ffn_final" if final else "ffn",
    )(*args)


def _even_kernel(hc_ref, hp_ref, g_ref, win_ref, cw_ref, cb_ref, lng_ref, lnb_ref, sink_ref, bias_ref,
                 wout_ref, o_ref, cbuf0, cbuf1, kvbuf0, kvbuf1, qbuf0, qbuf1, mix, *, n_tiles, tiles_per_seq):
    s = pl.program_id(0)
    tm = TOKEN_TILE

    @pl.when(s == 0)
    def _():
        cbuf1[...] = jnp.zeros(cbuf1.shape, F32)
        kvbuf1[...] = jnp.zeros(kvbuf1.shape, BF16)
        qbuf1[...] = jnp.zeros(qbuf1.shape, BF16)

    g = g_ref[...]
    cb = cb_ref[...]
    lng = lng_ref[...]
    lnb = lnb_ref[...]
    lo = lax.broadcasted_iota(jnp.int32, (SUB_TILE, LANES), 1) < HEAD_DIM
    lo_q = lax.broadcasted_iota(jnp.int32, (BLOCK, LANES), 1) < HEAD_DIM
    group_pairs = SWA_HEADS // SWA_KV_HEADS // 2
    n_q = CONV_A_HALO // SUBLANES
    ones_rows = lax.broadcasted_iota(jnp.int32, (4 * BLOCK, LANES), 0) < 2 * BLOCK
    ones_lanes = lax.broadcasted_iota(jnp.int32, (4 * BLOCK, LANES), 1) < HEAD_DIM
    row_sum_w = jnp.where(ones_rows == ones_lanes, 1.0, 0.0).astype(BF16)

    seq_start = jnp.minimum(s, n_tiles - 1) % tiles_per_seq == 0
    first = jnp.where(jnp.maximum(s - 1, 0) % tiles_per_seq == 0, SWA_HEADS, 0)

    def run(cw, kw, qw, cr, kr, qr):
        def in_proj(r0):
            u = _rmsnorm(hc_ref[r0:r0 + SUB_TILE, :], g).astype(BF16)
            half = CONV_A_CH // 2
            for c0 in (0, half):
                val = _dot(u, win_ref[:, c0:c0 + half])
                gate = _dot(u, win_ref[:, CONV_A_CH + c0:CONV_A_CH + c0 + half])
                cw[CONV_A_HALO + r0:CONV_A_HALO + r0 + SUB_TILE, c0:c0 + half] = val * jax.nn.sigmoid(gate)
                yield
            q = _dot(u, win_ref[:, 1024:1536])
            qw[r0:r0 + SUB_TILE, :] = (q * (HEAD_DIM ** -0.5)).astype(BF16)
            yield
            kv = _dot(u, win_ref[:, 1536:1792])
            for base, col in ((0, 0), (4, LANES)):
                f = kv[:, col:col + LANES]
                r = pltpu.roll(f, HEAD_DIM, 1)
                parts = (jnp.where(lo, f, 0.0), jnp.where(lo, 0.0, r),
                         jnp.where(lo, r, 0.0), jnp.where(lo, 0.0, f))
                for n, part in enumerate(parts):
                    kw[BLOCK + r0:BLOCK + r0 + SUB_TILE,
                       (base + n) * LANES:(base + n + 1) * LANES] = part.astype(BF16)
            yield

        def scores(blk):
            r0 = blk * BLOCK
            out = []
            for pr in range(SWA_HEADS // 2):
                kvh = pr // group_pairs
                qp = qr[r0:r0 + BLOCK, pr * LANES:(pr + 1) * LANES]
                probs = []
                sink_p = []
                for pos in range(2):
                    h = 2 * pr + pos
                    c = (2 * kvh + pos) * LANES
                    bias = bias_ref[first + h] if blk == 0 else bias_ref[h]
                    sc = lax.dot_general(qp, kr[r0:r0 + 2 * BLOCK, c:c + LANES], _NT,
                                         preferred_element_type=F32) + bias
                    sk = sink_ref[h:h + 1, 0:1]
                    m = jnp.maximum(jnp.max(sc, axis=-1, keepdims=True), sk)
                    probs.append(jnp.exp(sc - m).astype(BF16))
                    sink_p.append(jnp.exp(sk - m))
                out.append((jnp.concatenate(probs, axis=-1), jnp.where(lo_q, sink_p[0], sink_p[1])))
            return out

        def pv(blk, pairs):
            r0 = blk * BLOCK
            for pr, (pcat, sink_term) in enumerate(pairs):
                c = (4 + 2 * (pr // group_pairs)) * LANES
                vcat = jnp.concatenate([kr[r0:r0 + 2 * BLOCK, c:c + LANES],
                                        kr[r0:r0 + 2 * BLOCK, c + LANES:c + 2 * LANES]], axis=0)
                ol = _dot(pcat, jnp.concatenate([vcat, row_sum_w], axis=1))
                o = ol[:, 0:LANES] / (ol[:, LANES:2 * LANES] + sink_term)
                mix[r0:r0 + BLOCK, CONV_A_CH + pr * LANES:CONV_A_CH + (pr + 1) * LANES] = o.astype(BF16)

        def conv(r0):
            cols = []
            for c0 in range(0, CONV_A_CH, LANES):
                win = cr[r0:r0 + CONV_A_HALO + CONV_ROWS, c0:c0 + LANES]
                y = jnp.broadcast_to(cb[:, c0:c0 + LANES], (CONV_ROWS, LANES))
                for r in range(SUBLANES):
                    part = None
                    for q in range(n_q):
                        d = SUBLANES * q + r
                        if d >= CONV_A_WIDTH:
                            continue
                        k = CONV_A_WIDTH - 1 - d
                        s0 = CONV_A_HALO - SUBLANES * (q + 1)
                        term = cw_ref[k:k + 1, c0:c0 + LANES] * win[s0:s0 + SUBLANES + CONV_ROWS, :]
                        part = term if part is None else part + term
                    y = y + part[SUBLANES - r:SUBLANES - r + CONV_ROWS, :]
                cols.append(y)
            acc = jnp.concatenate(cols, axis=-1)
            mu = jnp.mean(acc, axis=-1, keepdims=True)
            cen = acc - mu
            var = jnp.mean(cen * cen, axis=-1, keepdims=True)
            y = cen * lax.rsqrt(var + LN_EPS) * lng + lnb
            mix[r0:r0 + CONV_ROWS, 0:CONV_A_CH] = jax.nn.silu(y).astype(BF16)

        def out_proj(r0):
            o_ref[r0:r0 + SUB_TILE, :] = (hp_ref[r0:r0 + SUB_TILE, :]
                                          + _dot(mix[r0:r0 + SUB_TILE, :], wout_ref[...]))

        cw[0:CONV_A_HALO, :] = jnp.where(seq_start, 0.0, cr[tm:tm + CONV_A_HALO, :])
        kw[0:BLOCK, :] = kr[tm:tm + BLOCK, :]

        n_blk = tm // BLOCK
        blk_per_sub = SUB_TILE // BLOCK
        conv_rows = list(range(0, tm, CONV_ROWS))
        conv_per_blk = len(conv_rows) // n_blk

        pending = scores(0)
        for blk in range(n_blk):
            if blk % blk_per_sub == 0:
                pieces = in_proj(blk // blk_per_sub * SUB_TILE)
            for _ in range(conv_per_blk):
                next(pieces)
                conv(conv_rows.pop(0))
            if blk + 1 < n_blk:
                cur = scores(blk + 1)
                pv(blk, pending)
                pending = cur
            else:
                pv(blk, pending)
            if (blk + 1) % blk_per_sub == 0:
                out_proj((blk + 1 - blk_per_sub) * BLOCK)

    @pl.when(s % 2 == 0)
    def _():
        run(cbuf0, kvbuf0, qbuf0, cbuf1, kvbuf1, qbuf1)

    @pl.when(s % 2 == 1)
    def _():
        run(cbuf1, kvbuf1, qbuf1, cbuf0, kvbuf0, qbuf0)


def _swa_bias():
    i = jnp.arange(BLOCK)[:, None]
    jj = jnp.arange(2 * BLOCK)[None, :]
    dist = i + BLOCK - jj
    valid = (dist >= 0) & (dist < BLOCK)
    slopes = 2.0 ** (-8.0 * jnp.arange(1, SWA_HEADS + 1, dtype=F32) / SWA_HEADS)
    b = -slopes[:, None, None] * dist.astype(F32)[None]
    normal = jnp.where(valid[None], b, -jnp.inf)
    first = jnp.where((valid & (jj >= BLOCK))[None], b, -jnp.inf)
    return jnp.concatenate([normal, first], axis=0)


def _even_mixer(h, layer, g, jl, w_in, cw, cb, lng, lnb, sinks, w_out, batch):
    t = h.shape[0]
    n_tiles = t // TOKEN_TILE
    cur = pl.BlockSpec((TOKEN_TILE, D_MODEL), lambda s: (jnp.minimum(s, n_tiles - 1), 0))
    prev = pl.BlockSpec((TOKEN_TILE, D_MODEL), lambda s: (jnp.maximum(s - 1, 0), 0))
    slot = [pltpu.VMEM((CONV_A_HALO + TOKEN_TILE, CONV_A_CH), F32)] * 2
    slot += [pltpu.VMEM((BLOCK + TOKEN_TILE, 8 * LANES), BF16)] * 2
    slot += [pltpu.VMEM((TOKEN_TILE, SWA_HEADS * HEAD_DIM), BF16)] * 2
    return pl.pallas_call(
        functools.partial(_even_kernel, n_tiles=n_tiles, tiles_per_seq=n_tiles // batch),
        grid=(n_tiles + 1,),
        in_specs=[cur, prev, _layer((1, D_MODEL), layer), _layer((D_MODEL, EVEN_IN), jl),
                  _layer((CONV_A_WIDTH, CONV_A_CH), jl), _layer((1, CONV_A_CH), jl),
                  _layer((1, CONV_A_CH), jl), _layer((1, CONV_A_CH), jl),
                  _layer((SWA_HEADS, LANES), jl), _resident((2 * SWA_HEADS, BLOCK, 2 * BLOCK)),
                  _layer((D_MODEL, D_MODEL), jl)],
        out_specs=prev,
        out_shape=jax.ShapeDtypeStruct((t, D_MODEL), F32),
        scratch_shapes=slot + [pltpu.VMEM((TOKEN_TILE, D_MODEL), BF16)],
        compiler_params=_params(1),
        name="even_mixer",
    )(h, h, _rows(g), w_in, cw, _rows(cb), _rows(lng), _rows(lnb),
      jnp.broadcast_to(sinks[:, :, None], sinks.shape + (LANES,)), _swa_bias(), w_out)


def _odd_kernel(h_ref, g_ref, win_ref, cw_ref, wout_ref, o_ref, cbuf):
    j = pl.program_id(1)
    tm = TOKEN_TILE

    @pl.when(j == 0)
    def _():
        cbuf[0:SC_HALO, :] = jnp.zeros((SC_HALO, SC_CH), F32)

    x = h_ref[...]
    u = _rmsnorm(x, g_ref[...]).astype(BF16)
    z = _dot(u, win_ref[...])
    cbuf[SC_HALO:SC_HALO + tm, :] = z[:, SC_CH:2 * SC_CH] * z[:, 2 * SC_CH:3 * SC_CH]
    y = None
    for d in range(SC_WIDTH):
        k = SC_WIDTH - 1 - d
        term = cw_ref[k:k + 1, :] * cbuf[SC_HALO - d:SC_HALO - d + tm, :]
        y = term if y is None else y + term
    cbuf[0:SC_HALO, :] = cbuf[tm:tm + SC_HALO, :]
    o_ref[...] = x + _dot((z[:, 0:SC_CH] * y).astype(BF16), wout_ref[...])


def _odd_mixer(h, layer, g, jl, w_in, cw, w_out, batch):
    t = h.shape[0]
    nj = t // batch // TOKEN_TILE
    row = pl.BlockSpec((TOKEN_TILE, D_MODEL), lambda b, j: (b * nj + j, 0))
    return pl.pallas_call(
        _odd_kernel,
        grid=(batch, nj),
        in_specs=[row, _layer((1, D_MODEL), layer), _layer((D_MODEL, 3 * SC_CH), jl),
                  _layer((SC_WIDTH, SC_CH), jl), _layer((SC_CH, D_MODEL), jl)],
        out_specs=row,
        out_shape=jax.ShapeDtypeStruct((t, D_MODEL), F32),
        scratch_shapes=[pltpu.VMEM((SC_HALO + TOKEN_TILE, SC_CH), F32)],
        compiler_params=_params(2),
        name="odd_mixer",
    )(h, _rows(g), w_in, cw, w_out)


def _memkv_kernel(m_ref, g_ref, wkv_ref, o_ref):
    o_ref[...] = _dot(_rmsnorm(m_ref[...], g_ref[...]).astype(BF16), wkv_ref[...]).astype(BF16)


def _mem_kv(mem2d, g, wkv):
    rows = mem2d.shape[0]
    depth = wkv.shape[0]
    return pl.pallas_call(
        _memkv_kernel,
        grid=(depth, rows // MEM_LEN),
        in_specs=[pl.BlockSpec((MEM_LEN, D_MODEL), lambda l, b: (b, 0)),
                  pl.BlockSpec((None, 1, D_MODEL), lambda l, b: (l, 0, 0)),
                  pl.BlockSpec((None, D_MODEL, 2 * D_MODEL), lambda l, b: (l, 0, 0))],
        out_specs=pl.BlockSpec((None, MEM_LEN, 2 * D_MODEL), lambda l, b: (l, b, 0)),
        out_shape=jax.ShapeDtypeStruct((depth, rows, 2 * D_MODEL), BF16),
        compiler_params=_params(2),
        name="mem_kv",
    )(mem2d, _rows(g), wkv)


def _xa_kernel(h_ref, g_ref, wq_ref, kv_ref, wo_ref, o_ref):
    x = h_ref[...]
    u = _rmsnorm(x, g_ref[...]).astype(BF16)
    q = (_dot(u, wq_ref[...]) * (XA_HEAD_DIM ** -0.5)).astype(BF16)
    outs = []
    for h in range(XA_HEADS):
        c = h * XA_HEAD_DIM
        s = lax.dot_general(q[:, c:c + XA_HEAD_DIM], kv_ref[:, c:c + XA_HEAD_DIM], _NT,
                            preferred_element_type=F32)
        m = jnp.max(s, axis=-1, keepdims=True)
        p = jnp.exp(s - m)
        l = jnp.sum(p, axis=-1, keepdims=True)
        o = _dot(p.astype(BF16), kv_ref[:, D_MODEL + c:D_MODEL + c + XA_HEAD_DIM])
        outs.append((o * (1.0 / l)).astype(BF16))
    o_ref[...] = x + _dot(jnp.concatenate(outs, axis=-1), wo_ref[...])


def _cross_attention(h, layer, g, wq, kv, wo, batch):
    t = h.shape[0]
    nj = t // batch // TOKEN_TILE
    row = pl.BlockSpec((TOKEN_TILE, D_MODEL), lambda b, j: (b * nj + j, 0))
    return pl.pallas_call(
        _xa_kernel,
        grid=(batch, nj),
        in_specs=[row, _layer((1, D_MODEL), layer), _layer((D_MODEL, D_MODEL), layer),
                  pl.BlockSpec((None, MEM_LEN, 2 * D_MODEL), lambda b, j: (layer, b, 0)),
                  _layer((D_MODEL, D_MODEL), layer)],
        out_specs=row,
        out_shape=jax.ShapeDtypeStruct((t, D_MODEL), F32),
        compiler_params=_params(2),
        name="cross_attention",
    )(h, _rows(g), wq, kv, wo)


def kernel(x, mem, ffn1_norm, ffn1_w_gu, ffn1_w_down, mix_norm, even_w_in, conv_a_w, conv_a_b,
           conv_a_ln_g, conv_a_ln_b, swa_sinks, even_w_out, odd_w_in, sc_conv_w, odd_w_out,
           xa_norm, xa_mem_norm, xa_wq, xa_wkv, xa_wo, ffn2_norm, ffn2_w_gu, ffn2_w_down, final_norm):
    batch, seq, d = x.shape
    depth = ffn1_norm.shape[0]
    assert d == D_MODEL and seq % TOKEN_TILE == 0 and TOKEN_TILE % BLOCK == 0
    assert mem.shape[1] == MEM_LEN
    h = x.reshape(batch * seq, d)
    mem2d = mem.reshape(batch * MEM_LEN, d)
    bf = lambda w: w.astype(BF16)
    ffn1_w_gu, ffn1_w_down, ffn2_w_gu, ffn2_w_down = map(bf, (ffn1_w_gu, ffn1_w_down, ffn2_w_gu, ffn2_w_down))
    even_w_in, even_w_out, odd_w_in, odd_w_out = map(bf, (even_w_in, even_w_out, odd_w_in, odd_w_out))
    xa_wq, xa_wo = bf(xa_wq), bf(xa_wo)
    kv = _mem_kv(mem2d, xa_mem_norm, bf(xa_wkv))
    for i in range(depth):
        j = i // 2
        h = _ffn(h, i, ffn1_norm, ffn1_w_gu, ffn1_w_down)
        if i % 2 == 0:
            h = _even_mixer(h, i, mix_norm, j, even_w_in, conv_a_w, conv_a_b, conv_a_ln_g, conv_a_ln_b,
                            swa_sinks, even_w_out, batch)
        else:
            h = _odd_mixer(h, i, mix_norm, j, odd_w_in, sc_conv_w, odd_w_out, batch)
        h = _cross_attention(h, i, xa_norm, xa_wq, kv, xa_wo, batch)
        h = _ffn(h, i, ffn2_norm, ffn2_w_gu, ffn2_w_down,
                 final_g=final_norm if i == depth - 1 else None)
    return h.reshape(batch, seq, d)
```
